```python
import jax, jax.numpy as jnp
from jax import lax
import numpy as np

D_MODEL = 1024
BATCH = 32
SEQ = 256
DEPTH = 2
DEC_BATCH = 8
DEC_SEQ = 4096
PAST_LEN = 256

GRID_W = 64
N_MIXERS = 2
N_CONV_LAYERS = (DEPTH + 1) // 2
N_ATTN_LAYERS = DEPTH // 2
HEAD_DIM = 128
N_HEADS = D_MODEL // HEAD_DIM
N_KV_HEADS = 2
GROUP = N_HEADS // N_KV_HEADS
Q_DIM = N_HEADS * HEAD_DIM
KV_DIM = N_KV_HEADS * HEAD_DIM
QKV_DIM = Q_DIM + 2 * KV_DIM
AXIS_DIM = HEAD_DIM // 2
ROPE_THETA = 10000.0
Q_BLOCK = 128
ATTN_SCALE = HEAD_DIM ** -0.5
CONV_WIDTH = 31
FFN_DIM = 2816
FFN_CONV_WIDTH = 3
N_MOD = 6
EPS = 1e-6

kernel_name = "hybrid_conv_gqa_diffusion_step"


def rmsnorm(x, g):
    x32 = x.astype(jnp.float32)
    y = x32 * lax.rsqrt(jnp.mean(x32 * x32, axis=-1, keepdims=True) + EPS)
    return (y * g.astype(jnp.float32)).astype(x.dtype)


def layernorm(x, g, b):
    x32 = x.astype(jnp.float32)
    mu = jnp.mean(x32, axis=-1, keepdims=True)
    xc = x32 - mu
    y = xc * lax.rsqrt(jnp.mean(xc * xc, axis=-1, keepdims=True) + EPS)
    return (y * g.astype(jnp.float32) + b.astype(jnp.float32)).astype(x.dtype)


def modulate(h, shift, scale):
    return h * (1 + scale) + shift


def dwconv(x, w, b):
    C = x.shape[-1]
    y = lax.conv_general_dilated(x, w[:, None, :].astype(x.dtype), window_strides=(1,), padding='SAME',
                                 dimension_numbers=('NWC', 'WIO', 'NWC'), feature_group_count=C)
    return y + b


def axial_rope(x):
    L = x.shape[1]
    rows = L // GRID_W
    row = jnp.repeat(jnp.arange(rows), GRID_W)
    col = jnp.tile(jnp.arange(GRID_W), rows)
    freqs = ROPE_THETA ** (-jnp.arange(0, AXIS_DIM, 2, dtype=jnp.float32) / AXIS_DIM)

    def rot(xp, pos):
        ang = pos.astype(jnp.float32)[:, None] * freqs[None, :]
        cos = jnp.concatenate([jnp.cos(ang), jnp.cos(ang)], -1)[None, :, None, :]
        sin = jnp.concatenate([jnp.sin(ang), jnp.sin(ang)], -1)[None, :, None, :]
        x1, x2 = jnp.split(xp, 2, axis=-1)
        return xp * cos + jnp.concatenate([-x2, x1], -1) * sin

    x32 = x.astype(jnp.float32)
    out = jnp.concatenate([rot(x32[..., :AXIS_DIM], row), rot(x32[..., AXIS_DIM:], col)], -1)
    return out.astype(x.dtype)


def attend(q, k, v):
    B, S = q.shape[0], q.shape[1]
    nb = S // Q_BLOCK
    qb = q.reshape(B, nb, Q_BLOCK, N_KV_HEADS, GROUP, HEAD_DIM).transpose(1, 0, 2, 3, 4, 5)

    def block(qblk):
        s = jnp.einsum('bqkgd,btkd->bkgqt', qblk, k).astype(jnp.float32) * ATTN_SCALE
        p = jax.nn.softmax(s, axis=-1).astype(v.dtype)
        return jnp.einsum('bkgqt,btkd->bqkgd', p, v)

    o = lax.map(block, qb)
    return o.transpose(1, 0, 2, 3, 4, 5).reshape(B, S, Q_DIM)


def gqa_mixer(h, w_qkv, q_g, k_g, w_o, ctx_k, ctx_v):
    B, L, _ = h.shape
    qkv = h @ w_qkv
    q, k, v = jnp.split(qkv, [Q_DIM, Q_DIM + KV_DIM], axis=-1)
    q = rmsnorm(q.reshape(B, L, N_HEADS, HEAD_DIM), q_g)
    k = rmsnorm(k.reshape(B, L, N_KV_HEADS, HEAD_DIM), k_g)
    v = v.reshape(B, L, N_KV_HEADS, HEAD_DIM)
    if ctx_k is None:
        k_all, v_all = k, v
    else:
        q = axial_rope(q)
        k_all = jnp.concatenate([axial_rope(k), ctx_k], axis=1)
        v_all = jnp.concatenate([v, ctx_v], axis=1)
    return attend(q, k_all, v_all) @ w_o, k, v


def conformer_conv(h, w_pw1, dw_w, dw_b, ln_g, ln_b, w_pw2):
    a, g = jnp.split(h @ w_pw1, 2, axis=-1)
    u = a * jax.nn.sigmoid(g)
    u = dwconv(u, dw_w, dw_b)
    u = jax.nn.silu(layernorm(u, ln_g, ln_b))
    return u @ w_pw2


def conv_ffn(h, w_up, conv_w, conv_b, w_down):
    gate, val = jnp.split(h @ w_up, 2, axis=-1)
    gate = dwconv(gate, conv_w, conv_b)
    return (jax.nn.silu(gate) * val) @ w_down


def setup_inputs(seed: int = 0) -> dict:
    key = jax.random.key(seed)
    ks = jax.random.split(key, 32)
    f32 = jnp.float32

    def nrm(k, shape, scale):
        return jax.random.normal(k, shape, f32) * scale

    return {
        'x_prompt': nrm(ks[0], (BATCH, SEQ, D_MODEL), 1.0),
        'x_sample': nrm(ks[1], (DEC_BATCH, DEC_SEQ, D_MODEL), 1.0),
        'cache_attn_k': nrm(ks[2], (DEC_BATCH, N_ATTN_LAYERS, PAST_LEN, N_KV_HEADS, HEAD_DIM), 1.0),
        'cache_attn_v': nrm(ks[3], (DEC_BATCH, N_ATTN_LAYERS, PAST_LEN, N_KV_HEADS, HEAD_DIM), 1.0),
        'c': nrm(ks[4], (DEC_BATCH, D_MODEL), 1.0),
        'c_ctx': nrm(ks[5], (D_MODEL,), 1.0),
        'ada_w': nrm(ks[6], (DEPTH, D_MODEL, N_MOD * D_MODEL), 0.5 * D_MODEL ** -0.5),
        'ada_b': nrm(ks[7], (DEPTH, N_MOD * D_MODEL), 0.02),
        'norm1_g': 1.0 + nrm(ks[8], (DEPTH, D_MODEL), 0.05),
        'norm2_g': 1.0 + nrm(ks[9], (DEPTH, D_MODEL), 0.05),
        'conv_w_pw1': nrm(ks[10], (N_CONV_LAYERS, D_MODEL, 2 * D_MODEL), D_MODEL ** -0.5),
        'conv_dw_w': nrm(ks[11], (N_CONV_LAYERS, CONV_WIDTH, D_MODEL), CONV_WIDTH ** -0.5),
        'conv_dw_b': nrm(ks[12], (N_CONV_LAYERS, D_MODEL), 0.02),
        'conv_ln_g': 1.0 + nrm(ks[13], (N_CONV_LAYERS, D_MODEL), 0.05),
        'conv_ln_b': nrm(ks[14], (N_CONV_LAYERS, D_MODEL), 0.02),
        'conv_w_pw2': nrm(ks[15], (N_CONV_LAYERS, D_MODEL, D_MODEL), D_MODEL ** -0.5),
        'attn_w_qkv': nrm(ks[16], (N_ATTN_LAYERS, D_MODEL, QKV_DIM), D_MODEL ** -0.5),
        'attn_q_g': 1.0 + nrm(ks[17], (N_ATTN_LAYERS, HEAD_DIM), 0.05),
        'attn_k_g': 1.0 + nrm(ks[18], (N_ATTN_LAYERS, HEAD_DIM), 0.05),
        'attn_w_o': nrm(ks[19], (N_ATTN_LAYERS, Q_DIM, D_MODEL), Q_DIM ** -0.5),
        'ffn_w_up': nrm(ks[20], (DEPTH, D_MODEL, 2 * FFN_DIM), D_MODEL ** -0.5),
        'ffn_conv_w': nrm(ks[21], (DEPTH, FFN_CONV_WIDTH, FFN_DIM), FFN_CONV_WIDTH ** -0.5),
        'ffn_conv_b': nrm(ks[22], (DEPTH, FFN_DIM), 0.02),
        'ffn_w_down': nrm(ks[23], (DEPTH, FFN_DIM, D_MODEL), FFN_DIM ** -0.5),
        'final_g': 1.0 + nrm(ks[24], (D_MODEL,), 0.05),
    }


def reference(x_prompt, x_sample, cache_attn_k, cache_attn_v, c, c_ctx,
              ada_w, ada_b, norm1_g, norm2_g,
              conv_w_pw1, conv_dw_w, conv_dw_b, conv_ln_g, conv_ln_b, conv_w_pw2,
              attn_w_qkv, attn_q_g, attn_k_g, attn_w_o,
              ffn_w_up, ffn_conv_w, ffn_conv_b, ffn_w_down, final_g):

    def trunk(x, cond, ctx_k_all, ctx_v_all):
        ks_out, vs_out = [], []
        for i in range(DEPTH):
            mod = (jax.nn.silu(cond) @ ada_w[i] + ada_b[i])[:, None, :]
            sh1, sc1, g1, sh2, sc2, g2 = jnp.split(mod, N_MOD, axis=-1)
            h = modulate(rmsnorm(x, norm1_g[i]), sh1, sc1)
            j = i // N_MIXERS
            if i % N_MIXERS == 0:
                y = conformer_conv(h, conv_w_pw1[j], conv_dw_w[j], conv_dw_b[j],
                                   conv_ln_g[j], conv_ln_b[j], conv_w_pw2[j])
            else:
                ck = None if ctx_k_all is None else ctx_k_all[:, j]
                cv = None if ctx_v_all is None else ctx_v_all[:, j]
                y, k, v = gqa_mixer(h, attn_w_qkv[j], attn_q_g[j], attn_k_g[j], attn_w_o[j], ck, cv)
                ks_out.append(k)
                vs_out.append(v)
            x = x + g1 * y
            h = modulate(rmsnorm(x, norm2_g[i]), sh2, sc2)
            x = x + g2 * conv_ffn(h, ffn_w_up[i], ffn_conv_w[i], ffn_conv_b[i], ffn_w_down[i])
        return rmsnorm(x, final_g), ks_out, vs_out

    y_prompt, ctx_ks, ctx_vs = trunk(x_prompt, c_ctx[None, :], None, None)
    new_k = jnp.stack(ctx_ks, axis=1)
    new_v = jnp.stack(ctx_vs, axis=1)

    y_sample, _, _ = trunk(x_sample, c, cache_attn_k, cache_attn_v)

    return (y_prompt, y_sample, new_k, new_v)
```

```python
import functools

import jax
import jax.numpy as jnp
from jax import lax
from jax.experimental import pallas as pl
from jax.experimental.pallas import tpu as pltpu

D_MODEL = 1024
DEPTH = 2
GRID_W = 64
HEAD_DIM = 128
N_HEADS = D_MODEL // HEAD_DIM
N_KV_HEADS = 2
GROUP = N_HEADS // N_KV_HEADS
Q_DIM = N_HEADS * HEAD_DIM
KV_DIM = N_KV_HEADS * HEAD_DIM
QKV_DIM = Q_DIM + 2 * KV_DIM
AXIS_DIM = HEAD_DIM // 2
ROPE_THETA = 10000.0
ATTN_SCALE = HEAD_DIM ** -0.5
CONV_WIDTH = 31
FFN_DIM = 2816
FFN_CONV_WIDTH = 3
N_MOD = 6
EPS = 1e-6
LOG2E = 1.4426950408889634

F32 = jnp.float32
BF16 = jnp.bfloat16

V7X_VMEM_BYTES = 64 * 1024 * 1024
LANES = 128
BF16_SUBLANES = 16
COND_ROWS = 16
HALO = BF16_SUBLANES
ROW_CHUNK = 32
CONV_ROW_CHUNK = 128
FFN_COL_CHUNK = 512
VMEM_LIMIT = V7X_VMEM_BYTES - 8 * 1024 * 1024


def _silu(x):
    return x * jax.nn.sigmoid(x)


def _cparams(n_axes):
    return pltpu.CompilerParams(
        dimension_semantics=("arbitrary",) * n_axes,
        vmem_limit_bytes=VMEM_LIMIT,
    )


def _adaln_kernel(c_ref, w_ref, b_ref, o_ref):
    s = _silu(c_ref[...]).astype(BF16)
    w = w_ref[0].astype(BF16)
    o_ref[0] = jnp.dot(s, w, preferred_element_type=F32) + b_ref[0]


def _adaln(cond, ada_w, ada_b):
    tn = D_MODEL
    n_out = N_MOD * D_MODEL
    return pl.pallas_call(
        _adaln_kernel,
        grid=(DEPTH, n_out // tn),
        in_specs=[
            pl.BlockSpec((COND_ROWS, D_MODEL), lambda l, n: (0, 0)),
            pl.BlockSpec((1, D_MODEL, tn), lambda l, n: (l, 0, n)),
            pl.BlockSpec((1, 1, tn), lambda l, n: (l, 0, n)),
        ],
        out_specs=pl.BlockSpec((1, COND_ROWS, tn), lambda l, n: (l, 0, n)),
        out_shape=jax.ShapeDtypeStruct((DEPTH, COND_ROWS, n_out), F32),
        compiler_params=_cparams(2),
        name="adaln_mod",
    )(cond, ada_w, ada_b.reshape(DEPTH, 1, n_out))


def _mod_slices(mod_ref, first):
    sh = mod_ref[0, :, first * D_MODEL:(first + 1) * D_MODEL]
    sc = 1.0 + mod_ref[0, :, (first + 1) * D_MODEL:(first + 2) * D_MODEL]
    gate = mod_ref[0, :, (first + 2) * D_MODEL:(first + 3) * D_MODEL]
    return sh, sc, gate


def _norm_modulate(x, g, sc, sh):
    ms = jnp.mean(x * x, axis=-1, keepdims=True)
    return (x * lax.rsqrt(ms + EPS) * g) * sc + sh


def _halo_specs(block_cols, tl, n_halo_blocks):
    per_tile = tl // HALO
    main = pl.BlockSpec((1, tl, block_cols), lambda b, t: (b, t, 0))
    prev = pl.BlockSpec((1, HALO, block_cols),
                        lambda b, t: (b, jnp.maximum(t * per_tile - 1, 0), 0))
    nxt = pl.BlockSpec((1, HALO, block_cols),
                       lambda b, t: (b, jnp.minimum((t + 1) * per_tile, n_halo_blocks - 1), 0))
    return [main, prev, nxt]


def _full_spec(shape):
    return pl.BlockSpec(shape, lambda b, t: (0,) * len(shape))


def _mod_spec(per_batch):
    if per_batch:
        return pl.BlockSpec((1, 1, N_MOD * D_MODEL), lambda b, t: (b, 0, 0))
    return pl.BlockSpec((1, 1, N_MOD * D_MODEL), lambda b, t: (0, 0, 0))


def _zero_edge_halo(h_scr, tl, n_tiles):
    t = pl.program_id(1)

    @pl.when(t == 0)
    def _():
        h_scr[0:HALO, :] = jnp.zeros((HALO, D_MODEL), BF16)

    @pl.when(t == n_tiles - 1)
    def _():
        h_scr[HALO + tl:, :] = jnp.zeros((HALO, D_MODEL), BF16)


def _conv_kernel(x_ref, xp_ref, xn_ref, mod_ref, n1g_ref, wpw1_ref, dww_ref, dwb_ref,
                 lng_ref, lnb_ref, wpw2_ref, o_ref, h_scr, u_scr, c_scr, z_scr, *, tl, n_tiles):
    sh, sc, gate = _mod_slices(mod_ref, 0)
    g = n1g_ref[...]
    h_scr[0:HALO, :] = _norm_modulate(xp_ref[0], g, sc, sh).astype(BF16)
    h_scr[HALO + tl:, :] = _norm_modulate(xn_ref[0], g, sc, sh).astype(BF16)

    def norm_body(r, carry):
        base = pl.multiple_of(r * ROW_CHUNK, ROW_CHUNK)
        xv = x_ref[0, pl.ds(base, ROW_CHUNK), :]
        h_scr[pl.ds(HALO + base, ROW_CHUNK), :] = _norm_modulate(xv, g, sc, sh).astype(BF16)
        return carry

    lax.fori_loop(0, tl // ROW_CHUNK, norm_body, 0)
    _zero_edge_halo(h_scr, tl, n_tiles)

    n_slabs = D_MODEL // LANES
    ag = jnp.dot(h_scr[...], wpw1_ref[...], preferred_element_type=F32)
    for j in range(n_slabs):
        ls = slice(j * LANES, (j + 1) * LANES)
        u_scr[j] = ag[:, ls] * jax.nn.sigmoid(ag[:, D_MODEL + j * LANES:D_MODEL + (j + 1) * LANES])

    tap0 = HALO - CONV_WIDTH // 2
    half = CONV_ROW_CHUNK // 2
    for j in range(n_slabs):
        ls = slice(j * LANES, (j + 1) * LANES)

        def conv_body(r, carry, j=j, ls=ls):
            base = r * CONV_ROW_CHUNK
            bias = jnp.broadcast_to(dwb_ref[:, ls], (half, LANES))
            acc = [bias, bias]
            for k in range(CONV_WIDTH):
                wk = dww_ref[k:k + 1, ls]
                for par in range(2):
                    acc[par] = acc[par] + u_scr[j, pl.ds(base + par + tap0 + k, half, stride=2), :] * wk
            for par in range(2):
                c_scr[j, pl.ds(base + par, half, stride=2), :] = acc[par]
            return carry

        lax.fori_loop(0, tl // CONV_ROW_CHUNK, conv_body, 0)

    def ln_body(r, carry):
        base = pl.multiple_of(r * ROW_CHUNK, ROW_CHUNK)
        cv = [c_scr[j, pl.ds(base, ROW_CHUNK), :] for j in range(n_slabs)]
        mu = jnp.sum(sum(cv), axis=-1, keepdims=True) * (1.0 / D_MODEL)
        xc = [v - mu for v in cv]
        var = jnp.sum(sum(v * v for v in xc), axis=-1, keepdims=True) * (1.0 / D_MODEL)
        rstd = lax.rsqrt(var + EPS)
        for j in range(n_slabs):
            ls = slice(j * LANES, (j + 1) * LANES)
            y = xc[j] * rstd * lng_ref[:, ls] + lnb_ref[:, ls]
            z_scr[pl.ds(base, ROW_CHUNK), ls] = _silu(y).astype(BF16)
        return carry

    lax.fori_loop(0, tl // ROW_CHUNK, ln_body, 0)

    y = jnp.dot(z_scr[...], wpw2_ref[...], preferred_element_type=F32)
    o_ref[0] = x_ref[0] + gate * y


def _conv_mixer(x, mod, n1g, wpw1, dww, dwb, lng, lnb, wpw2, *, tl):
    b, l, _ = x.shape
    n_tiles = l // tl
    rows = tl + 2 * HALO
    kern = functools.partial(_conv_kernel, tl=tl, n_tiles=n_tiles)
    return pl.pallas_call(
        kern,
        grid=(b, n_tiles),
        in_specs=_halo_specs(D_MODEL, tl, l // HALO) + [
            _mod_spec(mod.shape[0] > 1),
            _full_spec((1, D_MODEL)),
            _full_spec((D_MODEL, 2 * D_MODEL)),
            _full_spec((CONV_WIDTH, D_MODEL)),
            _full_spec((1, D_MODEL)),
            _full_spec((1, D_MODEL)),
            _full_spec((1, D_MODEL)),
            _full_spec((D_MODEL, D_MODEL)),
        ],
        out_specs=pl.BlockSpec((1, tl, D_MODEL), lambda b, t: (b, t, 0)),
        out_shape=jax.ShapeDtypeStruct(x.shape, F32),
        scratch_shapes=[
            pltpu.VMEM((rows, D_MODEL), BF16),
            pltpu.VMEM((D_MODEL // LANES, rows, LANES), F32),
            pltpu.VMEM((D_MODEL // LANES, tl, LANES), F32),
            pltpu.VMEM((tl, D_MODEL), BF16),
        ],
        compiler_params=_cparams(2),
        name="conv_mixer",
    )(x, x, x, mod, n1g, wpw1, dww, dwb, lng, lnb, wpw2)


def _ffn_kernel(*refs, tl, n_tiles, has_attn, final):
    refs = list(refs)
    x_ref, xp_ref, xn_ref = refs[:3]
    refs = refs[3:]
    if has_attn:
        a_ref, ap_ref, an_ref, wo_ref = refs[:4]
        refs = refs[4:]
    mod_ref, n2g_ref, wup_ref, cw_ref, cb_ref, wdn_ref = refs[:6]
    refs = refs[6:]
    if final:
        fg_ref = refs[0]
        refs = refs[1:]
    o_ref = refs[0]
    refs = refs[1:]
    x1_scr, h_scr, g_scr, a_scr = refs[:4]
    if has_attn:
        ao_scr = refs[4]

    rows = tl + 2 * HALO
    _, _, gate1 = _mod_slices(mod_ref, 0)
    sh, sc, gate2 = _mod_slices(mod_ref, 3)

    x1_scr[0:HALO, :] = xp_ref[0]
    x1_scr[HALO:HALO + tl, :] = x_ref[0]
    x1_scr[HALO + tl:, :] = xn_ref[0]
    if has_attn:
        ao_scr[0:HALO, :] = ap_ref[0]
        ao_scr[HALO:HALO + tl, :] = a_ref[0]
        ao_scr[HALO + tl:, :] = an_ref[0]
        y = jnp.dot(ao_scr[...], wo_ref[...], preferred_element_type=F32)
        x1_scr[...] = x1_scr[...] + gate1 * y

    g = n2g_ref[...]

    def norm_body(r, carry):
        base = pl.multiple_of(r * ROW_CHUNK, ROW_CHUNK)
        xv = x1_scr[pl.ds(base, ROW_CHUNK), :]
        h_scr[pl.ds(base, ROW_CHUNK), :] = _norm_modulate(xv, g, sc, sh).astype(BF16)
        return carry

    lax.fori_loop(0, rows // ROW_CHUNK, norm_body, 0)
    _zero_edge_halo(h_scr, tl, n_tiles)

    c0 = 0
    while c0 < FFN_DIM:
        cw = min(FFN_COL_CHUNK, FFN_DIM - c0)
        g_scr[:, 0:cw] = jnp.dot(h_scr[...], wup_ref[:, c0:c0 + cw], preferred_element_type=F32)
        val = jnp.dot(h_scr[HALO:HALO + tl, :], wup_ref[:, FFN_DIM + c0:FFN_DIM + c0 + cw],
                      preferred_element_type=F32)
        gc = cb_ref[:, c0:c0 + cw]
        for k in range(FFN_CONV_WIDTH):
            r0 = HALO - FFN_CONV_WIDTH // 2 + k
            gc = gc + g_scr[r0:r0 + tl, 0:cw] * cw_ref[k:k + 1, c0:c0 + cw]
        a_scr[:, c0:c0 + cw] = (_silu(gc) * val).astype(BF16)
        c0 += cw

    y = jnp.dot(a_scr[...], wdn_ref[...], preferred_element_type=F32)
    out = x1_scr[HALO:HALO + tl, :] + gate2 * y
    if final:
        ms = jnp.mean(out * out, axis=-1, keepdims=True)
        out = out * lax.rsqrt(ms + EPS) * fg_ref[...]
    o_ref[0] = out


def _conv_ffn(x, attn, wo, mod, n2g, wup, cw, cb, wdn, final_g, *, tl):
    b, l, _ = x.shape
    n_tiles = l // tl
    rows = tl + 2 * HALO
    has_attn = attn is not None
    final = final_g is not None
    kern = functools.partial(_ffn_kernel, tl=tl, n_tiles=n_tiles, has_attn=has_attn, final=final)
    in_specs = _halo_specs(D_MODEL, tl, l // HALO)
    args = [x, x, x]
    if has_attn:
        in_specs += _halo_specs(D_MODEL, tl, l // HALO) + [_full_spec((Q_DIM, D_MODEL))]
        args += [attn, attn, attn, wo]
    in_specs += [
        _mod_spec(mod.shape[0] > 1),
        _full_spec((1, D_MODEL)),
        _full_spec((D_MODEL, 2 * FFN_DIM)),
        _full_spec((FFN_CONV_WIDTH, FFN_DIM)),
        _full_spec((1, FFN_DIM)),
        _full_spec((FFN_DIM, D_MODEL)),
    ]
    args += [mod, n2g, wup, cw, cb, wdn]
    if final:
        in_specs.append(_full_spec((1, D_MODEL)))
        args.append(final_g)
    scratch = [
        pltpu.VMEM((rows, D_MODEL), F32),
        pltpu.VMEM((rows, D_MODEL), BF16),
        pltpu.VMEM((rows, FFN_COL_CHUNK), F32),
        pltpu.VMEM((tl, FFN_DIM), BF16),
    ]
    if has_attn:
        scratch.append(pltpu.VMEM((rows, D_MODEL), BF16))
    return pl.pallas_call(
        kern,
        grid=(b, n_tiles),
        in_specs=in_specs,
        out_specs=pl.BlockSpec((1, tl, D_MODEL), lambda b, t: (b, t, 0)),
        out_shape=jax.ShapeDtypeStruct(x.shape, F32),
        scratch_shapes=scratch,
        compiler_params=_cparams(2),
        name="conv_ffn_attn" if has_attn else "conv_ffn",
    )(*args)


def _head_norm(xh, g):
    ms = jnp.mean(xh * xh, axis=-1, keepdims=True)
    return xh * lax.rsqrt(ms + EPS) * g


def _rope(xh, cos, sin_lo, sin_hi):
    half = AXIS_DIM // 2
    return (xh * cos + pltpu.roll(xh, HEAD_DIM - half, 1) * sin_lo
            + pltpu.roll(xh, half, 1) * sin_hi)


def _qkv_kernel(*refs, rope):
    x_ref, mod_ref, n1g_ref, w_ref, qg_ref, kg_ref = refs[:6]
    refs = refs[6:]
    if rope:
        cos_ref, slo_ref, shi_ref = refs[:3]
        refs = refs[3:]
    q_ref, k_ref, v_ref = refs[:3]
    if not rope:
        k32_ref, v32_ref = refs[3:5]

    sh, sc, _ = _mod_slices(mod_ref, 0)
    h = _norm_modulate(x_ref[0], n1g_ref[...], sc, sh).astype(BF16)
    qkv = jnp.dot(h, w_ref[...], preferred_element_type=F32)
    if rope:
        cos, slo, shi = cos_ref[...], slo_ref[...], shi_ref[...]
    qg = qg_ref[...]
    kg = kg_ref[...]
    for i in range(N_HEADS):
        qh = _head_norm(qkv[:, i * HEAD_DIM:(i + 1) * HEAD_DIM], qg)
        if rope:
            qh = _rope(qh, cos, slo, shi)
        q_ref[0, i] = (qh * (ATTN_SCALE * LOG2E)).astype(BF16)
    for j in range(N_KV_HEADS):
        ls = slice(j * HEAD_DIM, (j + 1) * HEAD_DIM)
        kh = _head_norm(qkv[:, Q_DIM + j * HEAD_DIM:Q_DIM + (j + 1) * HEAD_DIM], kg)
        if rope:
            k_ref[0, :, ls] = _rope(kh, cos, slo, shi).astype(BF16)
        else:
            k32_ref[0, :, ls] = kh
            k_ref[0, :, ls] = kh.astype(BF16)
    v = qkv[:, Q_DIM + KV_DIM:]
    v_ref[0] = v.astype(BF16)
    if not rope:
        v32_ref[0] = v


def _qkv(x, mod, n1g, w, qg, kg, rope_tabs, *, tm):
    b, l, _ = x.shape
    rope = rope_tabs is not None
    in_specs = [
        pl.BlockSpec((1, tm, D_MODEL), lambda b, t: (b, t, 0)),
        _mod_spec(mod.shape[0] > 1),
        _full_spec((1, D_MODEL)),
        _full_spec((D_MODEL, QKV_DIM)),
        _full_spec((1, HEAD_DIM)),
        _full_spec((1, HEAD_DIM)),
    ]
    args = [x, mod, n1g, w, qg, kg]
    if rope:
        in_specs += [pl.BlockSpec((tm, HEAD_DIM), lambda b, t: (t, 0))] * 3
        args += list(rope_tabs)
    out_specs = [
        pl.BlockSpec((1, N_HEADS, tm, HEAD_DIM), lambda b, t: (b, 0, t, 0)),
        pl.BlockSpec((1, tm, KV_DIM), lambda b, t: (b, t, 0)),
        pl.BlockSpec((1, tm, KV_DIM), lambda b, t: (b, t, 0)),
    ]
    out_shape = [
        jax.ShapeDtypeStruct((b, N_HEADS, l, HEAD_DIM), BF16),
        jax.ShapeDtypeStruct((b, l, KV_DIM), BF16),
        jax.ShapeDtypeStruct((b, l, KV_DIM), BF16),
    ]
    if not rope:
        out_specs += [pl.BlockSpec((1, tm, KV_DIM), lambda b, t: (b, t, 0))] * 2
        out_shape += [jax.ShapeDtypeStruct((b, l, KV_DIM), F32)] * 2
    return pl.pallas_call(
        functools.partial(_qkv_kernel, rope=rope),
        grid=(b, l // tm),
        in_specs=in_specs,
        out_specs=out_specs,
        out_shape=out_shape,
        compiler_params=_cparams(2),
        name="qkv_rope" if rope else "qkv",
    )(*args)


def _rope_tables(l):
    pos = jnp.arange(l)
    row = (pos // GRID_W).astype(F32)
    col = (pos % GRID_W).astype(F32)
    freqs = ROPE_THETA ** (-jnp.arange(0, AXIS_DIM, 2, dtype=F32) / AXIS_DIM)
    zeros = jnp.zeros((l, AXIS_DIM // 2), F32)
    cos, slo, shi = [], [], []
    for p in (row, col):
        ang = p[:, None] * freqs[None, :]
        c, s = jnp.cos(ang), jnp.sin(ang)
        cos += [c, c]
        slo += [-s, zeros]
        shi += [zeros, s]
    return (jnp.concatenate(cos, -1), jnp.concatenate(slo, -1), jnp.concatenate(shi, -1))


def _attn_kernel(q_ref, k_ref, v_ref, o_ref, *, tq):
    for j in range(N_KV_HEADS):
        ls = slice(j * HEAD_DIM, (j + 1) * HEAD_DIM)
        kj = k_ref[0, :, ls]
        vj = v_ref[0, :, ls]
        qg = q_ref[0, j * GROUP:(j + 1) * GROUP].reshape(GROUP * tq, HEAD_DIM)
        s = lax.dot_general(qg, kj, (((1,), (1,)), ((), ())), preferred_element_type=F32)
        m = jnp.max(s, axis=-1, keepdims=True)
        p = jnp.exp2(s - m)
        l = jnp.sum(p, axis=-1, keepdims=True)
        o = jnp.dot(p.astype(BF16), vj, preferred_element_type=F32) / l
        for gi in range(GROUP):
            hs = slice((j * GROUP + gi) * HEAD_DIM, (j * GROUP + gi + 1) * HEAD_DIM)
            o_ref[0, :, hs] = o[gi * tq:(gi + 1) * tq].astype(BF16)


def _attention(q, k, v, *, tq):
    b, _, s, _ = q.shape
    t = k.shape[1]
    return pl.pallas_call(
        functools.partial(_attn_kernel, tq=tq),
        grid=(b, s // tq),
        in_specs=[
            pl.BlockSpec((1, N_HEADS, tq, HEAD_DIM), lambda b, i: (b, 0, i, 0)),
            pl.BlockSpec((1, t, KV_DIM), lambda b, i: (b, 0, 0)),
            pl.BlockSpec((1, t, KV_DIM), lambda b, i: (b, 0, 0)),
        ],
        out_specs=pl.BlockSpec((1, tq, Q_DIM), lambda b, i: (b, i, 0)),
        out_shape=jax.ShapeDtypeStruct((b, s, Q_DIM), BF16),
        compiler_params=_cparams(2),
        name="attention",
    )(q, k, v)


def _tile(l):
    return min(l, 512)


def kernel(x_prompt, x_sample, cache_attn_k, cache_attn_v, c, c_ctx, ada_w, ada_b, norm1_g, norm2_g,
           conv_w_pw1, conv_dw_w, conv_dw_b, conv_ln_g, conv_ln_b, conv_w_pw2,
           attn_w_qkv, attn_q_g, attn_k_g, attn_w_o,
           ffn_w_up, ffn_conv_w, ffn_conv_b, ffn_w_down, final_g):
    n_dec = c.shape[0]
    cond = jnp.concatenate(
        [c_ctx[None, :], c, jnp.zeros((COND_ROWS - 1 - n_dec, D_MODEL), F32)], axis=0)
    mod_all = _adaln(cond, ada_w, ada_b)

    def row(v):
        return v.reshape(1, -1)

    w_pw1 = conv_w_pw1[0].astype(BF16)
    w_pw2 = conv_w_pw2[0].astype(BF16)
    w_qkv = attn_w_qkv[0].astype(BF16)
    w_o = attn_w_o[0].astype(BF16)
    w_up = ffn_w_up.astype(BF16)
    w_down = ffn_w_down.astype(BF16)

    def trunk(x, mods, ctx_k, ctx_v):
        b, l, _ = x.shape
        tl = _tile(l)
        x = _conv_mixer(x, mods[0], row(norm1_g[0]), w_pw1, conv_dw_w[0], row(conv_dw_b[0]),
                        row(conv_ln_g[0]), row(conv_ln_b[0]), w_pw2, tl=tl)
        x = _conv_ffn(x, None, None, mods[0], row(norm2_g[0]), w_up[0], ffn_conv_w[0],
                      row(ffn_conv_b[0]), w_down[0], None, tl=tl)
        tabs = None if ctx_k is None else _rope_tables(l)
        outs = _qkv(x, mods[1], row(norm1_g[1]), w_qkv, row(attn_q_g[0]), row(attn_k_g[0]), tabs,
                    tm=tl)
        q, k, v = outs[:3]
        if ctx_k is not None:
            k = jnp.concatenate([k, ctx_k], axis=1)
            v = jnp.concatenate([v, ctx_v], axis=1)
        attn = _attention(q, k, v, tq=min(l, 128 if ctx_k is not None else 256))
        y = _conv_ffn(x, attn, w_o, mods[1], row(norm2_g[1]), w_up[1], ffn_conv_w[1],
                      row(ffn_conv_b[1]), w_down[1], row(final_g), tl=tl)
        return y, outs[3:]

    mods_p = [mod_all[i, 0:1].reshape(1, 1, -1) for i in range(DEPTH)]
    mods_s = [mod_all[i, 1:1 + n_dec].reshape(n_dec, 1, -1) for i in range(DEPTH)]

    y_prompt, (k32, v32) = trunk(x_prompt, mods_p, None, None)
    bp, lp, _ = x_prompt.shape
    new_k = k32.reshape(bp, 1, lp, N_KV_HEADS, HEAD_DIM)
    new_v = v32.reshape(bp, 1, lp, N_KV_HEADS, HEAD_DIM)

    past = cache_attn_k.shape[2]
    ctx_k = cache_attn_k[:, 0].reshape(n_dec, past, KV_DIM).astype(BF16)
    ctx_v = cache_attn_v[:, 0].reshape(n_dec, past, KV_DIM).astype(BF16)
    y_sample, _ = trunk(x_sample, mods_s, ctx_k, ctx_v)
    return (y_prompt, y_sample, new_k, new_v)
```

```python
import functools

import jax
import jax.numpy as jnp
from jax import lax
from jax.experimental import pallas as pl
from jax.experimental.pallas import tpu as pltpu

D_MODEL = 1024
DEPTH = 2
GRID_W = 64
HEAD_DIM = 128
N_HEADS = D_MODEL // HEAD_DIM
N_KV_HEADS = 2
GROUP = N_HEADS // N_KV_HEADS
Q_DIM = N_HEADS * HEAD_DIM
KV_DIM = N_KV_HEADS * HEAD_DIM
QKV_DIM = Q_DIM + 2 * KV_DIM
AXIS_DIM = HEAD_DIM // 2
ROPE_THETA = 10000.0
ATTN_SCALE = HEAD_DIM ** -0.5
CONV_WIDTH = 31
FFN_DIM = 2816
FFN_CONV_WIDTH = 3
N_MOD = 6
EPS = 1e-6
LOG2E = 1.4426950408889634

F32 = jnp.float32
BF16 = jnp.bfloat16

V7X_VMEM_BYTES = 64 * 1024 * 1024
LANES = 128
BF16_SUBLANES = 16
COND_ROWS = 16
HALO = BF16_SUBLANES
ROW_CHUNK = 32
CONV_ROW_CHUNK = 64
FFN_COL_CHUNK = 512
ATTN_KEY_CHUNK = 256
VMEM_LIMIT = V7X_VMEM_BYTES - 8 * 1024 * 1024


def _silu(x):
    return x * jax.nn.sigmoid(x)


def _cparams(n_axes):
    return pltpu.CompilerParams(
        dimension_semantics=("arbitrary",) * n_axes,
        vmem_limit_bytes=VMEM_LIMIT,
    )


def _adaln_kernel(c_ref, w_ref, b_ref, o_ref):
    s = _silu(c_ref[...]).astype(BF16)
    w = w_ref[0].astype(BF16)
    o_ref[0] = jnp.dot(s, w, preferred_element_type=F32) + b_ref[0]


def _adaln(cond, ada_w, ada_b):
    tn = D_MODEL
    n_out = N_MOD * D_MODEL
    return pl.pallas_call(
        _adaln_kernel,
        grid=(DEPTH, n_out // tn),
        in_specs=[
            pl.BlockSpec((COND_ROWS, D_MODEL), lambda l, n: (0, 0)),
            pl.BlockSpec((1, D_MODEL, tn), lambda l, n: (l, 0, n)),
            pl.BlockSpec((1, 1, tn), lambda l, n: (l, 0, n)),
        ],
        out_specs=pl.BlockSpec((1, COND_ROWS, tn), lambda l, n: (l, 0, n)),
        out_shape=jax.ShapeDtypeStruct((DEPTH, COND_ROWS, n_out), F32),
        compiler_params=_cparams(2),
        name="adaln_mod",
    )(cond, ada_w, ada_b.reshape(DEPTH, 1, n_out))


def _mod_slices(mod_ref, first):
    sh = mod_ref[0, :, first * D_MODEL:(first + 1) * D_MODEL]
    sc = 1.0 + mod_ref[0, :, (first + 1) * D_MODEL:(first + 2) * D_MODEL]
    gate = mod_ref[0, :, (first + 2) * D_MODEL:(first + 3) * D_MODEL]
    return sh, sc, gate


def _norm_modulate(x, g, sc, sh):
    ms = jnp.mean(x * x, axis=-1, keepdims=True)
    return (x * lax.rsqrt(ms + EPS) * g) * sc + sh


def _halo_specs(block_cols, tl, n_halo_blocks):
    per_tile = tl // HALO
    main = pl.BlockSpec((1, tl, block_cols), lambda b, t: (b, t, 0))
    prev = pl.BlockSpec((1, HALO, block_cols),
                        lambda b, t: (b, jnp.maximum(t * per_tile - 1, 0), 0))
    nxt = pl.BlockSpec((1, HALO, block_cols),
                       lambda b, t: (b, jnp.minimum((t + 1) * per_tile, n_halo_blocks - 1), 0))
    return [main, prev, nxt]


def _full_spec(shape):
    return pl.BlockSpec(shape, lambda b, t: (0,) * len(shape))


def _mod_spec(per_batch):
    if per_batch:
        return pl.BlockSpec((1, 1, N_MOD * D_MODEL), lambda b, t: (b, 0, 0))
    return pl.BlockSpec((1, 1, N_MOD * D_MODEL), lambda b, t: (0, 0, 0))


def _zero_edge_halo(h_scr, tl, n_tiles):
    t = pl.program_id(1)

    @pl.when(t == 0)
    def _():
        h_scr[0:HALO, :] = jnp.zeros((HALO, D_MODEL), BF16)

    @pl.when(t == n_tiles - 1)
    def _():
        h_scr[HALO + tl:, :] = jnp.zeros((HALO, D_MODEL), BF16)


def _conv_kernel(x_ref, xp_ref, xn_ref, xo_ref, modc_ref, modo_ref, n1g_ref, wpw1_ref, dww_ref,
                 dwb_ref, lng_ref, lnb_ref, wpw2_ref, o_ref, h_scr, u_scr, c_scr, z_scr,
                 *, tl, n_tiles):
    i = pl.program_id(0)
    n = pl.num_programs(0) - 2
    n_slabs = D_MODEL // LANES

    @pl.when(i == 0)
    def _():
        u_scr[...] = jnp.zeros(u_scr.shape, F32)
        z_scr[...] = jnp.zeros(z_scr.shape, BF16)

    _, _, gate_o = _mod_slices(modo_ref, 0)
    y = jnp.dot(z_scr[...], wpw2_ref[...], preferred_element_type=F32)
    o_ref[0] = xo_ref[0] + gate_o * y

    tap0 = HALO - CONV_WIDTH // 2
    half = CONV_ROW_CHUNK // 2
    for j in range(n_slabs):
        ls = slice(j * LANES, (j + 1) * LANES)
        for base in range(0, tl, CONV_ROW_CHUNK):
            bias = jnp.broadcast_to(dwb_ref[:, ls], (half, LANES))
            acc = [bias, bias]
            for k in range(CONV_WIDTH):
                wk = dww_ref[k:k + 1, ls]
                for par in range(2):
                    acc[par] = acc[par] + u_scr[j, pl.ds(base + par + tap0 + k, half, stride=2), :] * wk
            for par in range(2):
                c_scr[j, pl.ds(base + par, half, stride=2), :] = acc[par]

    cv = [c_scr[j] for j in range(n_slabs)]
    mu = jnp.sum(sum(cv), axis=-1, keepdims=True) * (1.0 / D_MODEL)
    xc = [v - mu for v in cv]
    var = jnp.sum(sum(v * v for v in xc), axis=-1, keepdims=True) * (1.0 / D_MODEL)
    rstd = lax.rsqrt(var + EPS)
    for j in range(n_slabs):
        ls = slice(j * LANES, (j + 1) * LANES)
        yj = xc[j] * rstd * lng_ref[:, ls] + lnb_ref[:, ls]
        z_scr[:, ls] = _silu(yj).astype(BF16)

    t_cur = lax.rem(jnp.minimum(i, n - 1), n_tiles)
    keep_first = (t_cur != 0).astype(F32)
    keep_last = (t_cur != n_tiles - 1).astype(F32)
    sh, sc, _ = _mod_slices(modc_ref, 0)
    g = n1g_ref[...]
    h_scr[0:HALO, :] = (_norm_modulate(xp_ref[0], g, sc, sh) * keep_first).astype(BF16)
    h_scr[HALO:HALO + tl, :] = _norm_modulate(x_ref[0], g, sc, sh).astype(BF16)
    h_scr[HALO + tl:, :] = (_norm_modulate(xn_ref[0], g, sc, sh) * keep_last).astype(BF16)
    ag = jnp.dot(h_scr[...], wpw1_ref[...], preferred_element_type=F32)
    for j in range(n_slabs):
        ls = slice(j * LANES, (j + 1) * LANES)
        u_scr[j] = ag[:, ls] * jax.nn.sigmoid(ag[:, D_MODEL + j * LANES:D_MODEL + (j + 1) * LANES])


def _conv_mixer(x, mod, n1g, wpw1, dww, dwb, lng, lnb, wpw2, *, tl):
    b, l, _ = x.shape
    n_tiles = l // tl
    n = b * n_tiles
    rows = tl + 2 * HALO
    per_tile = tl // HALO
    per_batch_mod = mod.shape[0] > 1

    def cur(i):
        u = jnp.minimum(i, n - 1)
        return u // n_tiles, u % n_tiles

    def out(i):
        u = jnp.maximum(i - 2, 0)
        return u // n_tiles, u % n_tiles

    def full(shape):
        return pl.BlockSpec(shape, lambda i: (0,) * len(shape))

    def mod_spec(which):
        if per_batch_mod:
            return pl.BlockSpec((1, 1, N_MOD * D_MODEL), lambda i: (which(i)[0], 0, 0))
        return full((1, 1, N_MOD * D_MODEL))

    out_spec = pl.BlockSpec((1, tl, D_MODEL), lambda i: (out(i)[0], out(i)[1], 0))
    in_specs = [
        pl.BlockSpec((1, tl, D_MODEL), lambda i: (cur(i)[0], cur(i)[1], 0)),
        pl.BlockSpec((1, HALO, D_MODEL),
                     lambda i: (cur(i)[0], jnp.maximum(cur(i)[1] * per_tile - 1, 0), 0)),
        pl.BlockSpec((1, HALO, D_MODEL),
                     lambda i: (cur(i)[0], jnp.minimum((cur(i)[1] + 1) * per_tile, l // HALO - 1), 0)),
        out_spec,
        mod_spec(cur),
        mod_spec(out),
        full((1, D_MODEL)),
        full((D_MODEL, 2 * D_MODEL)),
        full((CONV_WIDTH, D_MODEL)),
        full((1, D_MODEL)),
        full((1, D_MODEL)),
        full((1, D_MODEL)),
        full((D_MODEL, D_MODEL)),
    ]
    return pl.pallas_call(
        functools.partial(_conv_kernel, tl=tl, n_tiles=n_tiles),
        grid=(n + 2,),
        in_specs=in_specs,
        out_specs=out_spec,
        out_shape=jax.ShapeDtypeStruct(x.shape, F32),
        scratch_shapes=[
            pltpu.VMEM((rows, D_MODEL), BF16),
            pltpu.VMEM((D_MODEL // LANES, rows, LANES), F32),
            pltpu.VMEM((D_MODEL // LANES, tl, LANES), F32),
            pltpu.VMEM((tl, D_MODEL), BF16),
        ],
        compiler_params=_cparams(1),
        name="conv_mixer",
    )(x, x, x, x, mod, mod, n1g, wpw1, dww, dwb, lng, lnb, wpw2)


def _ffn_kernel(*refs, tl, n_tiles, has_attn, final):
    refs = list(refs)

    def take(n):
        head = refs[:n]
        del refs[:n]
        return head

    x_ref, xp_ref, xn_ref = take(3)
    if has_attn:
        a_ref, ap_ref, an_ref, wo_ref = take(4)
    else:
        (xo_ref,) = take(1)
    modc_ref, modo_ref, n2g_ref, wup_ref, cw_ref, cb_ref, wdn_ref = take(7)
    if final:
        (fg_ref,) = take(1)
    (o_ref,) = take(1)
    h_scr, g_scr, a_scr = take(3)
    if has_attn:
        ao_scr, x1_scr = take(2)

    i = pl.program_id(0)
    n = pl.num_programs(0) - 1
    t_cur = lax.rem(jnp.minimum(i, n - 1), n_tiles)
    keep_first = (t_cur != 0).astype(F32)
    keep_last = (t_cur != n_tiles - 1).astype(F32)

    def prepare_tile():
        sh, sc, _ = _mod_slices(modc_ref, 3)
        g = n2g_ref[...]
        xm, xp, xn = x_ref[0], xp_ref[0], xn_ref[0]
        if has_attn:
            _, _, gate1 = _mod_slices(modc_ref, 0)
            ao_scr[0:HALO, :] = ap_ref[0]
            ao_scr[HALO:HALO + tl, :] = a_ref[0]
            ao_scr[HALO + tl:, :] = an_ref[0]
            y = jnp.dot(ao_scr[...], wo_ref[...], preferred_element_type=F32)
            xp = xp + gate1 * y[0:HALO]
            xm = xm + gate1 * y[HALO:HALO + tl]
            xn = xn + gate1 * y[HALO + tl:]
            x1_scr[lax.rem(i, 2)] = xm
        h_scr[0:HALO, :] = (_norm_modulate(xp, g, sc, sh) * keep_first).astype(BF16)
        h_scr[HALO:HALO + tl, :] = _norm_modulate(xm, g, sc, sh).astype(BF16)
        h_scr[HALO + tl:, :] = (_norm_modulate(xn, g, sc, sh) * keep_last).astype(BF16)

    def up_conv_act():
        c0 = 0
        while c0 < FFN_DIM:
            cw = min(FFN_COL_CHUNK, FFN_DIM - c0)
            g_scr[:, c0:c0 + cw] = jnp.dot(h_scr[...], wup_ref[:, c0:c0 + cw],
                                           preferred_element_type=F32)
            val = jnp.dot(h_scr[HALO:HALO + tl, :], wup_ref[:, FFN_DIM + c0:FFN_DIM + c0 + cw],
                          preferred_element_type=F32)
            gc = cb_ref[:, c0:c0 + cw]
            for k in range(FFN_CONV_WIDTH):
                r0 = HALO - FFN_CONV_WIDTH // 2 + k
                gc = gc + g_scr[r0:r0 + tl, c0:c0 + cw] * cw_ref[k:k + 1, c0:c0 + cw]
            a_scr[:, c0:c0 + cw] = (_silu(gc) * val).astype(BF16)
            c0 += cw

    def down_residual():
        _, _, gate2 = _mod_slices(modo_ref, 3)
        y = jnp.dot(a_scr[...], wdn_ref[...], preferred_element_type=F32)
        res = x1_scr[1 - lax.rem(i, 2)] if has_attn else xo_ref[0]
        out = res + gate2 * y
        if final:
            ms = jnp.mean(out * out, axis=-1, keepdims=True)
            out = out * lax.rsqrt(ms + EPS) * fg_ref[...]
        o_ref[0] = out

    @pl.when(i == 0)
    def _():
        prepare_tile()
        o_ref[0] = jnp.zeros((tl, D_MODEL), F32)

    @pl.when(i > 0)
    def _():
        up_conv_act()
        prepare_tile()
        down_residual()


def _conv_ffn(x, attn, wo, mod, n2g, wup, cw, cb, wdn, final_g, *, tl):
    b, l, _ = x.shape
    n_tiles = l // tl
    n = b * n_tiles
    rows = tl + 2 * HALO
    per_tile = tl // HALO
    has_attn = attn is not None
    final = final_g is not None
    per_batch_mod = mod.shape[0] > 1

    def cur(i):
        u = jnp.minimum(i, n - 1)
        return u // n_tiles, u % n_tiles

    def prv(i):
        u = jnp.maximum(i - 1, 0)
        return u // n_tiles, u % n_tiles

    def halo_specs():
        main = pl.BlockSpec((1, tl, D_MODEL), lambda i: (cur(i)[0], cur(i)[1], 0))
        before = pl.BlockSpec(
            (1, HALO, D_MODEL),
            lambda i: (cur(i)[0], jnp.maximum(cur(i)[1] * per_tile - 1, 0), 0))
        after = pl.BlockSpec(
            (1, HALO, D_MODEL),
            lambda i: (cur(i)[0], jnp.minimum((cur(i)[1] + 1) * per_tile, l // HALO - 1), 0))
        return [main, before, after]

    def full(shape):
        return pl.BlockSpec(shape, lambda i: (0,) * len(shape))

    def mod_spec(which):
        if per_batch_mod:
            return pl.BlockSpec((1, 1, N_MOD * D_MODEL), lambda i: (which(i)[0], 0, 0))
        return full((1, 1, N_MOD * D_MODEL))

    out_spec = pl.BlockSpec((1, tl, D_MODEL), lambda i: (prv(i)[0], prv(i)[1], 0))
    kern = functools.partial(_ffn_kernel, tl=tl, n_tiles=n_tiles, has_attn=has_attn, final=final)
    in_specs = halo_specs()
    args = [x, x, x]
    if has_attn:
        in_specs += halo_specs() + [full((Q_DIM, D_MODEL))]
        args += [attn, attn, attn, wo]
    else:
        in_specs.append(out_spec)
        args.append(x)
    in_specs += [
        mod_spec(cur),
        mod_spec(prv),
        full((1, D_MODEL)),
        full((D_MODEL, 2 * FFN_DIM)),
        full((FFN_CONV_WIDTH, FFN_DIM)),
        full((1, FFN_DIM)),
        full((FFN_DIM, D_MODEL)),
    ]
    args += [mod, mod, n2g, wup, cw, cb, wdn]
    if final:
        in_specs.append(full((1, D_MODEL)))
        args.append(final_g)
    scratch = [
        pltpu.VMEM((rows, D_MODEL), BF16),
        pltpu.VMEM((rows, FFN_DIM), F32),
        pltpu.VMEM((tl, FFN_DIM), BF16),
    ]
    if has_attn:
        scratch += [pltpu.VMEM((rows, D_MODEL), BF16), pltpu.VMEM((2, tl, D_MODEL), F32)]
    return pl.pallas_call(
        kern,
        grid=(n + 1,),
        in_specs=in_specs,
        out_specs=out_spec,
        out_shape=jax.ShapeDtypeStruct(x.shape, F32),
        scratch_shapes=scratch,
        compiler_params=_cparams(1),
        name="conv_ffn_attn" if has_attn else "conv_ffn",
    )(*args)


def _head_norm(xh, g):
    ms = jnp.mean(xh * xh, axis=-1, keepdims=True)
    return xh * lax.rsqrt(ms + EPS) * g


def _rope(xh, cos, sin_lo, sin_hi):
    half = AXIS_DIM // 2
    return (xh * cos + pltpu.roll(xh, HEAD_DIM - half, 1) * sin_lo
            + pltpu.roll(xh, half, 1) * sin_hi)


def _qkv_kernel(*refs, rope):
    x_ref, mod_ref, n1g_ref, w_ref, qg_ref, kg_ref = refs[:6]
    refs = refs[6:]
    if rope:
        cos_ref, slo_ref, shi_ref = refs[:3]
        refs = refs[3:]
    q_ref, k_ref, v_ref = refs[:3]
    if not rope:
        k32_ref, v32_ref = refs[3:5]

    sh, sc, _ = _mod_slices(mod_ref, 0)
    h = _norm_modulate(x_ref[0], n1g_ref[...], sc, sh).astype(BF16)
    qkv = jnp.dot(h, w_ref[...], preferred_element_type=F32)
    if rope:
        cos, slo, shi = cos_ref[...], slo_ref[...], shi_ref[...]
    qg = qg_ref[...]
    kg = kg_ref[...]
    for i in range(N_HEADS):
        qh = _head_norm(qkv[:, i * HEAD_DIM:(i + 1) * HEAD_DIM], qg)
        if rope:
            qh = _rope(qh, cos, slo, shi)
        q_ref[0, i] = (qh * (ATTN_SCALE * LOG2E)).astype(BF16)
    for j in range(N_KV_HEADS):
        ls = slice(j * HEAD_DIM, (j + 1) * HEAD_DIM)
        kh = _head_norm(qkv[:, Q_DIM + j * HEAD_DIM:Q_DIM + (j + 1) * HEAD_DIM], kg)
        if rope:
            k_ref[0, :, ls] = _rope(kh, cos, slo, shi).astype(BF16)
        else:
            k32_ref[0, :, ls] = kh
            k_ref[0, :, ls] = kh.astype(BF16)
    v = qkv[:, Q_DIM + KV_DIM:]
    v_ref[0] = v.astype(BF16)
    if not rope:
        v32_ref[0] = v


def _qkv(x, mod, n1g, w, qg, kg, rope_tabs, *, tm):
    b, l, _ = x.shape
    rope = rope_tabs is not None
    in_specs = [
        pl.BlockSpec((1, tm, D_MODEL), lambda b, t: (b, t, 0)),
        _mod_spec(mod.shape[0] > 1),
        _full_spec((1, D_MODEL)),
        _full_spec((D_MODEL, QKV_DIM)),
        _full_spec((1, HEAD_DIM)),
        _full_spec((1, HEAD_DIM)),
    ]
    args = [x, mod, n1g, w, qg, kg]
    if rope:
        in_specs += [pl.BlockSpec((tm, HEAD_DIM), lambda b, t: (t, 0))] * 3
        args += list(rope_tabs)
    out_specs = [
        pl.BlockSpec((1, N_HEADS, tm, HEAD_DIM), lambda b, t: (b, 0, t, 0)),
        pl.BlockSpec((1, tm, KV_DIM), lambda b, t: (b, t, 0)),
        pl.BlockSpec((1, tm, KV_DIM), lambda b, t: (b, t, 0)),
    ]
    out_shape = [
        jax.ShapeDtypeStruct((b, N_HEADS, l, HEAD_DIM), BF16),
        jax.ShapeDtypeStruct((b, l, KV_DIM), BF16),
        jax.ShapeDtypeStruct((b, l, KV_DIM), BF16),
    ]
    if not rope:
        out_specs += [pl.BlockSpec((1, tm, KV_DIM), lambda b, t: (b, t, 0))] * 2
        out_shape += [jax.ShapeDtypeStruct((b, l, KV_DIM), F32)] * 2
    return pl.pallas_call(
        functools.partial(_qkv_kernel, rope=rope),
        grid=(b, l // tm),
        in_specs=in_specs,
        out_specs=out_specs,
        out_shape=out_shape,
        compiler_params=_cparams(2),
        name="qkv_rope" if rope else "qkv",
    )(*args)


def _rope_tables(l):
    pos = jnp.arange(l)
    row = (pos // GRID_W).astype(F32)
    col = (pos % GRID_W).astype(F32)
    freqs = ROPE_THETA ** (-jnp.arange(0, AXIS_DIM, 2, dtype=F32) / AXIS_DIM)
    zeros = jnp.zeros((l, AXIS_DIM // 2), F32)
    cos, slo, shi = [], [], []
    for p in (row, col):
        ang = p[:, None] * freqs[None, :]
        c, s = jnp.cos(ang), jnp.sin(ang)
        cos += [c, c]
        slo += [-s, zeros]
        shi += [zeros, s]
    return (jnp.concatenate(cos, -1), jnp.concatenate(slo, -1), jnp.concatenate(shi, -1))


def _attn_scores(q_ref, k_ref, s_w, m_w, *, tq):
    for j in range(N_KV_HEADS):
        ls = slice(j * HEAD_DIM, (j + 1) * HEAD_DIM)
        qg = q_ref[0, j * GROUP:(j + 1) * GROUP].reshape(GROUP * tq, HEAD_DIM)
        s = lax.dot_general(k_ref[0, :, ls], qg, (((1,), (1,)), ((), ())),
                            preferred_element_type=F32)
        s_w[j] = s
        m_w[j] = jnp.broadcast_to(jnp.max(s, axis=0, keepdims=True), m_w.shape[1:])


def _attn_softmax_pv(v_ref, o_ref, s_r, m_r, *, tq):
    n_keys = v_ref.shape[1]
    for j in range(N_KV_HEADS):
        ls = slice(j * HEAD_DIM, (j + 1) * HEAD_DIM)
        m = m_r[j, 0:1, :]
        l = jnp.zeros_like(m)
        ot = jnp.zeros((HEAD_DIM, GROUP * tq), F32)
        for c0 in range(0, n_keys, ATTN_KEY_CHUNK):
            cw = min(ATTN_KEY_CHUNK, n_keys - c0)
            p = jnp.exp2(s_r[j, c0:c0 + cw, :] - m)
            l = l + jnp.sum(p, axis=0, keepdims=True)
            ot = ot + lax.dot_general(v_ref[0, c0:c0 + cw, ls], p.astype(BF16),
                                      (((0,), (0,)), ((), ())),
                                      preferred_element_type=F32)
        o = (ot / l).T
        for gi in range(GROUP):
            hs = slice((j * GROUP + gi) * HEAD_DIM, (j * GROUP + gi + 1) * HEAD_DIM)
            o_ref[0, :, hs] = o[gi * tq:(gi + 1) * tq].astype(BF16)


def _attn_kernel(q_ref, k_ref, v_ref, o_ref, s0_scr, s1_scr, m0_scr, m1_scr, *, tq):
    i = pl.program_id(0)

    @pl.when(i == 0)
    def _():
        s1_scr[...] = jnp.zeros(s1_scr.shape, F32)
        m1_scr[...] = jnp.zeros(m1_scr.shape, F32)

    @pl.when(i % 2 == 0)
    def _():
        _attn_softmax_pv(v_ref, o_ref, s1_scr, m1_scr, tq=tq)
        _attn_scores(q_ref, k_ref, s0_scr, m0_scr, tq=tq)

    @pl.when(i % 2 == 1)
    def _():
        _attn_softmax_pv(v_ref, o_ref, s0_scr, m0_scr, tq=tq)
        _attn_scores(q_ref, k_ref, s1_scr, m1_scr, tq=tq)


def _attention(q, k, v, *, tq):
    b, _, s, _ = q.shape
    t = k.shape[1]
    n_q = s // tq
    n = b * n_q

    def cur(i):
        u = jnp.minimum(i, n - 1)
        return u // n_q, u % n_q

    def prev(i):
        u = jnp.maximum(i - 1, 0)
        return u // n_q, u % n_q

    s_shape = (N_KV_HEADS, t, GROUP * tq)
    m_shape = (N_KV_HEADS, 8, GROUP * tq)
    return pl.pallas_call(
        functools.partial(_attn_kernel, tq=tq),
        grid=(n + 1,),
        in_specs=[
            pl.BlockSpec((1, N_HEADS, tq, HEAD_DIM), lambda i: (cur(i)[0], 0, cur(i)[1], 0)),
            pl.BlockSpec((1, t, KV_DIM), lambda i: (cur(i)[0], 0, 0)),
            pl.BlockSpec((1, t, KV_DIM), lambda i: (prev(i)[0], 0, 0)),
        ],
        out_specs=pl.BlockSpec((1, tq, Q_DIM), lambda i: (prev(i)[0], prev(i)[1], 0)),
        out_shape=jax.ShapeDtypeStruct((b, s, Q_DIM), BF16),
        scratch_shapes=[pltpu.VMEM(s_shape, F32), pltpu.VMEM(s_shape, F32),
                        pltpu.VMEM(m_shape, F32), pltpu.VMEM(m_shape, F32)],
        compiler_params=_cparams(1),
        name="attention",
    )(q, k, v)


def _tile(l):
    return min(l, 512)


def kernel(x_prompt, x_sample, cache_attn_k, cache_attn_v, c, c_ctx, ada_w, ada_b, norm1_g, norm2_g,
           conv_w_pw1, conv_dw_w, conv_dw_b, conv_ln_g, conv_ln_b, conv_w_pw2,
           attn_w_qkv, attn_q_g, attn_k_g, attn_w_o,
           ffn_w_up, ffn_conv_w, ffn_conv_b, ffn_w_down, final_g):
    n_dec = c.shape[0]
    cond = jnp.concatenate(
        [c_ctx[None, :], c, jnp.zeros((COND_ROWS - 1 - n_dec, D_MODEL), F32)], axis=0)
    mod_all = _adaln(cond, ada_w, ada_b)

    def row(v):
        return v.reshape(1, -1)

    w_pw1 = conv_w_pw1[0].astype(BF16)
    w_pw2 = conv_w_pw2[0].astype(BF16)
    w_qkv = attn_w_qkv[0].astype(BF16)
    w_o = attn_w_o[0].astype(BF16)
    w_up = ffn_w_up.astype(BF16)
    w_down = ffn_w_down.astype(BF16)

    def trunk(x, mods, ctx_k, ctx_v):
        b, l, _ = x.shape
        tl = _tile(l)
        x = _conv_mixer(x, mods[0], row(norm1_g[0]), w_pw1, conv_dw_w[0], row(conv_dw_b[0]),
                        row(conv_ln_g[0]), row(conv_ln_b[0]), w_pw2, tl=tl)
        x = _conv_ffn(x, None, None, mods[0], row(norm2_g[0]), w_up[0], ffn_conv_w[0],
                      row(ffn_conv_b[0]), w_down[0], None, tl=tl)
        tabs = None if ctx_k is None else _rope_tables(l)
        outs = _qkv(x, mods[1], row(norm1_g[1]), w_qkv, row(attn_q_g[0]), row(attn_k_g[0]), tabs,
                    tm=tl)
        q, k, v = outs[:3]
        if ctx_k is not None:
            k = jnp.concatenate([k, ctx_k], axis=1)
            v = jnp.concatenate([v, ctx_v], axis=1)
        attn = _attention(q, k, v, tq=min(l, 128 if ctx_k is not None else 256))
        y = _conv_ffn(x, attn, w_o, mods[1], row(norm2_g[1]), w_up[1], ffn_conv_w[1],
                      row(ffn_conv_b[1]), w_down[1], row(final_g), tl=tl)
        return y, outs[3:]

    mods_p = [mod_all[i, 0:1].reshape(1, 1, -1) for i in range(DEPTH)]
    mods_s = [mod_all[i, 1:1 + n_dec].reshape(n_dec, 1, -1) for i in range(DEPTH)]

    y_prompt, (k32, v32) = trunk(x_prompt, mods_p, None, None)
    bp, lp, _ = x_prompt.shape
    new_k = k32.reshape(bp, 1, lp, N_KV_HEADS, HEAD_DIM)
    new_v = v32.reshape(bp, 1, lp, N_KV_HEADS, HEAD_DIM)

    past = cache_attn_k.shape[2]
    ctx_k = cache_attn_k[:, 0].reshape(n_dec, past, KV_DIM).astype(BF16)
    ctx_v = cache_attn_v[:, 0].reshape(n_dec, past, KV_DIM).astype(BF16)
    y_sample, _ = trunk(x_sample, mods_s, ctx_k, ctx_v)
    return (y_prompt, y_sample, new_k, new_v)
```

```python
import functools

import jax
import jax.numpy as jnp
import numpy as np
from jax import lax
from jax.experimental import pallas as pl
from jax.experimental.pallas import tpu as pltpu

D_MODEL = 1024
DEPTH = 2
GRID_W = 64
HEAD_DIM = 128
N_HEADS = D_MODEL // HEAD_DIM
N_KV_HEADS = 2
GROUP = N_HEADS // N_KV_HEADS
Q_DIM = N_HEADS * HEAD_DIM
KV_DIM = N_KV_HEADS * HEAD_DIM
QKV_DIM = Q_DIM + 2 * KV_DIM
AXIS_DIM = HEAD_DIM // 2
ROPE_THETA = 10000.0
ATTN_SCALE = HEAD_DIM ** -0.5
CONV_WIDTH = 31
FFN_DIM = 2816
FFN_CONV_WIDTH = 3
N_MOD = 6
EPS = 1e-6
LOG2E = 1.4426950408889634

F32 = jnp.float32
BF16 = jnp.bfloat16

V7X_VMEM_BYTES = 64 * 1024 * 1024
LANES = 128
BF16_SUBLANES = 16
COND_ROWS = 16
HALO = BF16_SUBLANES
CONV_ROW_CHUNK = 64
FFN_COL_CHUNK = 512
ATTN_KEY_CHUNK = 1024
VMEM_LIMIT = V7X_VMEM_BYTES - 8 * 1024 * 1024


def _silu(x):
    return x * jax.nn.sigmoid(x)


def _cparams(n_axes):
    return pltpu.CompilerParams(
        dimension_semantics=("arbitrary",) * n_axes,
        vmem_limit_bytes=VMEM_LIMIT,
    )


def _adaln_kernel(c_ref, w_ref, b_ref, o_ref):
    s = _silu(c_ref[...]).astype(BF16)
    w = w_ref[0].astype(BF16)
    o_ref[0] = jnp.dot(s, w, preferred_element_type=F32) + b_ref[0]


def _adaln(cond, ada_w, ada_b):
    tn = D_MODEL
    n_out = N_MOD * D_MODEL
    return pl.pallas_call(
        _adaln_kernel,
        grid=(DEPTH, n_out // tn),
        in_specs=[
            pl.BlockSpec((COND_ROWS, D_MODEL), lambda l, n: (0, 0)),
            pl.BlockSpec((1, D_MODEL, tn), lambda l, n: (l, 0, n)),
            pl.BlockSpec((1, 1, tn), lambda l, n: (l, 0, n)),
        ],
        out_specs=pl.BlockSpec((1, COND_ROWS, tn), lambda l, n: (l, 0, n)),
        out_shape=jax.ShapeDtypeStruct((DEPTH, COND_ROWS, n_out), F32),
        compiler_params=_cparams(2),
        name="adaln_mod",
    )(cond, ada_w, ada_b.reshape(DEPTH, 1, n_out))


def _mod_slices(mod_ref, first):
    sh = mod_ref[0, :, first * D_MODEL:(first + 1) * D_MODEL]
    sc = 1.0 + mod_ref[0, :, (first + 1) * D_MODEL:(first + 2) * D_MODEL]
    gate = mod_ref[0, :, (first + 2) * D_MODEL:(first + 3) * D_MODEL]
    return sh, sc, gate


def _norm_modulate(x, g, sc, sh):
    ms = jnp.mean(x * x, axis=-1, keepdims=True)
    return (x * lax.rsqrt(ms + EPS) * g) * sc + sh


def _conv_kernel(x_ref, xp_ref, xn_ref, xo_ref, modc_ref, modo_ref, n1g_ref, wpw1_ref, dww_ref,
                 dwb_ref, lng_ref, lnb_ref, wpw2_ref, o_ref, h_scr, u_scr, c_scr, z_scr,
                 *, tl, n_tiles):
    i = pl.program_id(0)
    n = pl.num_programs(0) - 2
    n_slabs = D_MODEL // LANES

    @pl.when(i == 0)
    def _():
        u_scr[...] = jnp.zeros(u_scr.shape, F32)
        z_scr[...] = jnp.zeros(z_scr.shape, BF16)

    _, _, gate_o = _mod_slices(modo_ref, 0)
    y = jnp.dot(z_scr[...], wpw2_ref[...], preferred_element_type=F32)
    o_ref[0] = xo_ref[0] + gate_o * y

    tap0 = HALO - CONV_WIDTH // 2
    half = CONV_ROW_CHUNK // 2
    for j in range(n_slabs):
        ls = slice(j * LANES, (j + 1) * LANES)
        for base in range(0, tl, CONV_ROW_CHUNK):
            bias = jnp.broadcast_to(dwb_ref[:, ls], (half, LANES))
            acc = [bias, bias]
            for k in range(CONV_WIDTH):
                wk = dww_ref[k:k + 1, ls]
                for par in range(2):
                    acc[par] = acc[par] + u_scr[j, pl.ds(base + par + tap0 + k, half, stride=2), :] * wk
            for par in range(2):
                c_scr[j, pl.ds(base + par, half, stride=2), :] = acc[par]

    cv = [c_scr[j] for j in range(n_slabs)]
    mu = jnp.sum(sum(cv), axis=-1, keepdims=True) * (1.0 / D_MODEL)
    xc = [v - mu for v in cv]
    var = jnp.sum(sum(v * v for v in xc), axis=-1, keepdims=True) * (1.0 / D_MODEL)
    rstd = lax.rsqrt(var + EPS)
    for j in range(n_slabs):
        ls = slice(j * LANES, (j + 1) * LANES)
        yj = xc[j] * rstd * lng_ref[:, ls] + lnb_ref[:, ls]
        z_scr[:, ls] = _silu(yj).astype(BF16)

    t_cur = lax.rem(jnp.minimum(i, n - 1), n_tiles)
    keep_first = (t_cur != 0).astype(F32)
    keep_last = (t_cur != n_tiles - 1).astype(F32)
    sh, sc, _ = _mod_slices(modc_ref, 0)
    g = n1g_ref[...]
    h_scr[0:HALO, :] = (_norm_modulate(xp_ref[0], g, sc, sh) * keep_first).astype(BF16)
    h_scr[HALO:HALO + tl, :] = _norm_modulate(x_ref[0], g, sc, sh).astype(BF16)
    h_scr[HALO + tl:, :] = (_norm_modulate(xn_ref[0], g, sc, sh) * keep_last).astype(BF16)
    ag = jnp.dot(h_scr[...], wpw1_ref[...], preferred_element_type=F32)
    for j in range(n_slabs):
        ls = slice(j * LANES, (j + 1) * LANES)
        u_scr[j] = ag[:, ls] * jax.nn.sigmoid(ag[:, D_MODEL + j * LANES:D_MODEL + (j + 1) * LANES])


def _conv_mixer(x, mod, n1g, wpw1, dww, dwb, lng, lnb, wpw2, *, tl):
    b, l, _ = x.shape
    n_tiles = l // tl
    n = b * n_tiles
    rows = tl + 2 * HALO
    per_tile = tl // HALO
    per_batch_mod = mod.shape[0] > 1

    def cur(i):
        u = jnp.minimum(i, n - 1)
        return u // n_tiles, u % n_tiles

    def out(i):
        u = jnp.maximum(i - 2, 0)
        return u // n_tiles, u % n_tiles

    def full(shape):
        return pl.BlockSpec(shape, lambda i: (0,) * len(shape))

    def mod_spec(which):
        if per_batch_mod:
            return pl.BlockSpec((1, 1, N_MOD * D_MODEL), lambda i: (which(i)[0], 0, 0))
        return full((1, 1, N_MOD * D_MODEL))

    out_spec = pl.BlockSpec((1, tl, D_MODEL), lambda i: (out(i)[0], out(i)[1], 0))
    in_specs = [
        pl.BlockSpec((1, tl, D_MODEL), lambda i: (cur(i)[0], cur(i)[1], 0)),
        pl.BlockSpec((1, HALO, D_MODEL),
                     lambda i: (cur(i)[0], jnp.maximum(cur(i)[1] * per_tile - 1, 0), 0)),
        pl.BlockSpec((1, HALO, D_MODEL),
                     lambda i: (cur(i)[0], jnp.minimum((cur(i)[1] + 1) * per_tile, l // HALO - 1), 0)),
        out_spec,
        mod_spec(cur),
        mod_spec(out),
        full((1, D_MODEL)),
        full((D_MODEL, 2 * D_MODEL)),
        full((CONV_WIDTH, D_MODEL)),
        full((1, D_MODEL)),
        full((1, D_MODEL)),
        full((1, D_MODEL)),
        full((D_MODEL, D_MODEL)),
    ]
    return pl.pallas_call(
        functools.partial(_conv_kernel, tl=tl, n_tiles=n_tiles),
        grid=(n + 2,),
        in_specs=in_specs,
        out_specs=out_spec,
        out_shape=jax.ShapeDtypeStruct(x.shape, F32),
        scratch_shapes=[
            pltpu.VMEM((rows, D_MODEL), BF16),
            pltpu.VMEM((D_MODEL // LANES, rows, LANES), F32),
            pltpu.VMEM((D_MODEL // LANES, tl, LANES), F32),
            pltpu.VMEM((tl, D_MODEL), BF16),
        ],
        compiler_params=_cparams(1),
        name="conv_mixer",
    )(x, x, x, x, mod, mod, n1g, wpw1, dww, dwb, lng, lnb, wpw2)


def _ffn_kernel(*refs, tl, n_tiles, has_attn, final):
    refs = list(refs)

    def take(n):
        head = refs[:n]
        del refs[:n]
        return head

    x_ref, xp_ref, xn_ref = take(3)
    if has_attn:
        a_ref, ap_ref, an_ref, wo_ref = take(4)
    else:
        (xo_ref,) = take(1)
    modc_ref, modo_ref, n2g_ref, wup_ref, cw_ref, cb_ref, wdn_ref = take(7)
    if final:
        (fg_ref,) = take(1)
    (o_ref,) = take(1)
    h_scr, g_scr, a_scr = take(3)
    if has_attn:
        ao_scr, x1_scr = take(2)

    i = pl.program_id(0)
    n = pl.num_programs(0) - 1
    t_cur = lax.rem(jnp.minimum(i, n - 1), n_tiles)
    keep_first = (t_cur != 0).astype(F32)
    keep_last = (t_cur != n_tiles - 1).astype(F32)

    def prepare_tile():
        sh, sc, _ = _mod_slices(modc_ref, 3)
        g = n2g_ref[...]
        xm, xp, xn = x_ref[0], xp_ref[0], xn_ref[0]
        if has_attn:
            _, _, gate1 = _mod_slices(modc_ref, 0)
            ao_scr[0:HALO, :] = ap_ref[0]
            ao_scr[HALO:HALO + tl, :] = a_ref[0]
            ao_scr[HALO + tl:, :] = an_ref[0]
            y = jnp.dot(ao_scr[...], wo_ref[...], preferred_element_type=F32)
            xp = xp + gate1 * y[0:HALO]
            xm = xm + gate1 * y[HALO:HALO + tl]
            xn = xn + gate1 * y[HALO + tl:]
            x1_scr[lax.rem(i, 2)] = xm
        h_scr[0:HALO, :] = (_norm_modulate(xp, g, sc, sh) * keep_first).astype(BF16)
        h_scr[HALO:HALO + tl, :] = _norm_modulate(xm, g, sc, sh).astype(BF16)
        h_scr[HALO + tl:, :] = (_norm_modulate(xn, g, sc, sh) * keep_last).astype(BF16)

    def up_conv_act():
        c0 = 0
        while c0 < FFN_DIM:
            cw = min(FFN_COL_CHUNK, FFN_DIM - c0)
            g_scr[:, c0:c0 + cw] = jnp.dot(h_scr[...], wup_ref[:, c0:c0 + cw],
                                           preferred_element_type=F32)
            val = jnp.dot(h_scr[HALO:HALO + tl, :], wup_ref[:, FFN_DIM + c0:FFN_DIM + c0 + cw],
                          preferred_element_type=F32)
            gc = cb_ref[:, c0:c0 + cw]
            for k in range(FFN_CONV_WIDTH):
                r0 = HALO - FFN_CONV_WIDTH // 2 + k
                gc = gc + g_scr[r0:r0 + tl, c0:c0 + cw] * cw_ref[k:k + 1, c0:c0 + cw]
            a_scr[:, c0:c0 + cw] = (_silu(gc) * val).astype(BF16)
            c0 += cw

    def down_residual():
        _, _, gate2 = _mod_slices(modo_ref, 3)
        y = jnp.dot(a_scr[...], wdn_ref[...], preferred_element_type=F32)
        res = x1_scr[1 - lax.rem(i, 2)] if has_attn else xo_ref[0]
        out = res + gate2 * y
        if final:
            ms = jnp.mean(out * out, axis=-1, keepdims=True)
            out = out * lax.rsqrt(ms + EPS) * fg_ref[...]
        o_ref[0] = out

    @pl.when(i == 0)
    def _():
        prepare_tile()
        o_ref[0] = jnp.zeros((tl, D_MODEL), F32)

    @pl.when(i > 0)
    def _():
        up_conv_act()
        prepare_tile()
        down_residual()


def _conv_ffn(x, attn, wo, mod, n2g, wup, cw, cb, wdn, final_g, *, tl):
    b, l, _ = x.shape
    n_tiles = l // tl
    n = b * n_tiles
    rows = tl + 2 * HALO
    per_tile = tl // HALO
    has_attn = attn is not None
    final = final_g is not None
    per_batch_mod = mod.shape[0] > 1

    def cur(i):
        u = jnp.minimum(i, n - 1)
        return u // n_tiles, u % n_tiles

    def prv(i):
        u = jnp.maximum(i - 1, 0)
        return u // n_tiles, u % n_tiles

    def halo_specs():
        main = pl.BlockSpec((1, tl, D_MODEL), lambda i: (cur(i)[0], cur(i)[1], 0))
        before = pl.BlockSpec(
            (1, HALO, D_MODEL),
            lambda i: (cur(i)[0], jnp.maximum(cur(i)[1] * per_tile - 1, 0), 0))
        after = pl.BlockSpec(
            (1, HALO, D_MODEL),
            lambda i: (cur(i)[0], jnp.minimum((cur(i)[1] + 1) * per_tile, l // HALO - 1), 0))
        return [main, before, after]

    def full(shape):
        return pl.BlockSpec(shape, lambda i: (0,) * len(shape))

    def mod_spec(which):
        if per_batch_mod:
            return pl.BlockSpec((1, 1, N_MOD * D_MODEL), lambda i: (which(i)[0], 0, 0))
        return full((1, 1, N_MOD * D_MODEL))

    out_spec = pl.BlockSpec((1, tl, D_MODEL), lambda i: (prv(i)[0], prv(i)[1], 0))
    kern = functools.partial(_ffn_kernel, tl=tl, n_tiles=n_tiles, has_attn=has_attn, final=final)
    in_specs = halo_specs()
    args = [x, x, x]
    if has_attn:
        in_specs += halo_specs() + [full((Q_DIM, D_MODEL))]
        args += [attn, attn, attn, wo]
    else:
        in_specs.append(out_spec)
        args.append(x)
    in_specs += [
        mod_spec(cur),
        mod_spec(prv),
        full((1, D_MODEL)),
        full((D_MODEL, 2 * FFN_DIM)),
        full((FFN_CONV_WIDTH, FFN_DIM)),
        full((1, FFN_DIM)),
        full((FFN_DIM, D_MODEL)),
    ]
    args += [mod, mod, n2g, wup, cw, cb, wdn]
    if final:
        in_specs.append(full((1, D_MODEL)))
        args.append(final_g)
    scratch = [
        pltpu.VMEM((rows, D_MODEL), BF16),
        pltpu.VMEM((rows, FFN_DIM), F32),
        pltpu.VMEM((tl, FFN_DIM), BF16),
    ]
    if has_attn:
        scratch += [pltpu.VMEM((rows, D_MODEL), BF16), pltpu.VMEM((2, tl, D_MODEL), F32)]
    return pl.pallas_call(
        kern,
        grid=(n + 1,),
        in_specs=in_specs,
        out_specs=out_spec,
        out_shape=jax.ShapeDtypeStruct(x.shape, F32),
        scratch_shapes=scratch,
        compiler_params=_cparams(1),
        name="conv_ffn_attn" if has_attn else "conv_ffn",
    )(*args)


def _head_norm(xh, g):
    ms = jnp.mean(xh * xh, axis=-1, keepdims=True)
    return xh * lax.rsqrt(ms + EPS) * g


def _rope(xh, cos, sin_lo, sin_hi):
    half = AXIS_DIM // 2
    return (xh * cos + pltpu.roll(xh, HEAD_DIM - half, 1) * sin_lo
            + pltpu.roll(xh, half, 1) * sin_hi)


def _qkv_kernel(*refs, rope):
    x_ref, mod_ref, n1g_ref, w_ref, qg_ref, kg_ref = refs[:6]
    refs = refs[6:]
    if rope:
        cos_ref, slo_ref, shi_ref = refs[:3]
        refs = refs[3:]
    q_ref, k_ref, v_ref = refs[:3]
    refs = refs[3:]
    if not rope:
        k32_ref, v32_ref = refs[:2]
        refs = refs[2:]
    (qkv_scr,) = refs
    tm = x_ref.shape[1]

    @pl.when(pl.program_id(0) == 0)
    def _():
        qkv_scr[...] = jnp.zeros(qkv_scr.shape, F32)

    if rope:
        cos, slo, shi = cos_ref[...], slo_ref[...], shi_ref[...]
    qg = qg_ref[...]
    kg = kg_ref[...]
    for i in range(N_HEADS):
        qh = _head_norm(qkv_scr[:, i * HEAD_DIM:(i + 1) * HEAD_DIM], qg)
        if rope:
            qh = _rope(qh, cos, slo, shi)
        q_ref[0, i] = (qh * (ATTN_SCALE * LOG2E)).astype(BF16)
    ones = jnp.ones((tm, HEAD_DIM), BF16)
    for j in range(N_KV_HEADS):
        ls = slice(j * HEAD_DIM, (j + 1) * HEAD_DIM)
        kh = _head_norm(qkv_scr[:, Q_DIM + j * HEAD_DIM:Q_DIM + (j + 1) * HEAD_DIM], kg)
        vh = qkv_scr[:, Q_DIM + KV_DIM + j * HEAD_DIM:Q_DIM + KV_DIM + (j + 1) * HEAD_DIM]
        if rope:
            k_ref[0, :, ls] = _rope(kh, cos, slo, shi).astype(BF16)
        else:
            k32_ref[0, :, ls] = kh
            v32_ref[0, :, ls] = vh
            k_ref[0, :, ls] = kh.astype(BF16)
        v_ref[0, :, 2 * j * HEAD_DIM:(2 * j + 1) * HEAD_DIM] = vh.astype(BF16)
        v_ref[0, :, (2 * j + 1) * HEAD_DIM:(2 * j + 2) * HEAD_DIM] = ones

    sh, sc, _ = _mod_slices(mod_ref, 0)
    h = _norm_modulate(x_ref[0], n1g_ref[...], sc, sh).astype(BF16)
    qkv_scr[...] = jnp.dot(h, w_ref[...], preferred_element_type=F32)


def _qkv(x, mod, n1g, w, qg, kg, rope_tabs, *, tm):
    b, l, _ = x.shape
    n_tiles = l // tm
    n = b * n_tiles
    rope = rope_tabs is not None

    def cur(i):
        u = jnp.minimum(i, n - 1)
        return u // n_tiles, u % n_tiles

    def prv(i):
        u = jnp.maximum(i - 1, 0)
        return u // n_tiles, u % n_tiles

    def full(shape):
        return pl.BlockSpec(shape, lambda i: (0,) * len(shape))

    if mod.shape[0] > 1:
        mod_spec = pl.BlockSpec((1, 1, N_MOD * D_MODEL), lambda i: (cur(i)[0], 0, 0))
    else:
        mod_spec = full((1, 1, N_MOD * D_MODEL))
    in_specs = [
        pl.BlockSpec((1, tm, D_MODEL), lambda i: (cur(i)[0], cur(i)[1], 0)),
        mod_spec,
        full((1, D_MODEL)),
        full((D_MODEL, QKV_DIM)),
        full((1, HEAD_DIM)),
        full((1, HEAD_DIM)),
    ]
    args = [x, mod, n1g, w, qg, kg]
    if rope:
        in_specs += [pl.BlockSpec((tm, HEAD_DIM), lambda i: (prv(i)[1], 0))] * 3
        args += list(rope_tabs)

    def tok_spec(cols):
        return pl.BlockSpec((1, tm, cols), lambda i: (prv(i)[0], prv(i)[1], 0))

    out_specs = [
        pl.BlockSpec((1, N_HEADS, tm, HEAD_DIM), lambda i: (prv(i)[0], 0, prv(i)[1], 0)),
        tok_spec(KV_DIM),
        tok_spec(2 * KV_DIM),
    ]
    out_shape = [
        jax.ShapeDtypeStruct((b, N_HEADS, l, HEAD_DIM), BF16),
        jax.ShapeDtypeStruct((b, l, KV_DIM), BF16),
        jax.ShapeDtypeStruct((b, l, 2 * KV_DIM), BF16),
    ]
    if not rope:
        out_specs += [tok_spec(KV_DIM)] * 2
        out_shape += [jax.ShapeDtypeStruct((b, l, KV_DIM), F32)] * 2
    return pl.pallas_call(
        functools.partial(_qkv_kernel, rope=rope),
        grid=(n + 1,),
        in_specs=in_specs,
        out_specs=out_specs,
        out_shape=out_shape,
        scratch_shapes=[pltpu.VMEM((tm, QKV_DIM), F32)],
        compiler_params=_cparams(1),
        name="qkv_rope" if rope else "qkv",
    )(*args)


def _rope_tables(l):
    pos = np.arange(l)
    row = (pos // GRID_W).astype(np.float32)
    col = (pos % GRID_W).astype(np.float32)
    freqs = (ROPE_THETA ** (-np.arange(0, AXIS_DIM, 2, dtype=np.float32) / AXIS_DIM)).astype(np.float32)
    zeros = np.zeros((l, AXIS_DIM // 2), np.float32)
    cos, slo, shi = [], [], []
    for p in (row, col):
        ang = (p[:, None] * freqs[None, :]).astype(np.float32)
        c, s = np.cos(ang), np.sin(ang)
        cos += [c, c]
        slo += [-s, zeros]
        shi += [zeros, s]
    return tuple(jnp.asarray(np.concatenate(t, -1), F32) for t in (cos, slo, shi))


def _attn_scores(q_ref, k_refs, s_w, *, tq):
    for j in range(N_KV_HEADS):
        ls = slice(j * HEAD_DIM, (j + 1) * HEAD_DIM)
        qg = q_ref[0, j * GROUP:(j + 1) * GROUP].reshape(GROUP * tq, HEAD_DIM)
        off = 0
        for k_ref in k_refs:
            nk = k_ref.shape[1]
            s_w[j, :, off:off + nk] = lax.dot_general(
                qg, k_ref[0, :, ls], (((1,), (1,)), ((), ())), preferred_element_type=F32)
            off += nk


def _attn_softmax_pv(v_refs, o_ref, s_r, *, tq):
    for j in range(N_KV_HEADS):
        vs = slice(2 * j * HEAD_DIM, 2 * (j + 1) * HEAD_DIM)
        m = jnp.max(s_r[j], axis=-1, keepdims=True)
        acc = jnp.zeros((GROUP * tq, 2 * HEAD_DIM), F32)
        off = 0
        for v_ref in v_refs:
            nk = v_ref.shape[1]
            for c0 in range(0, nk, ATTN_KEY_CHUNK):
                cw = min(ATTN_KEY_CHUNK, nk - c0)
                p = jnp.exp2(s_r[j, :, off + c0:off + c0 + cw] - m)
                acc = acc + jnp.dot(p.astype(BF16), v_ref[0, c0:c0 + cw, vs],
                                    preferred_element_type=F32)
            off += nk
        o = acc[:, :HEAD_DIM] / acc[:, HEAD_DIM:HEAD_DIM + 1]
        for gi in range(GROUP):
            hs = slice((j * GROUP + gi) * HEAD_DIM, (j * GROUP + gi + 1) * HEAD_DIM)
            o_ref[0, :, hs] = o[gi * tq:(gi + 1) * tq].astype(BF16)


def _attn_kernel(*refs, tq, n_src):
    q_ref = refs[0]
    k_refs = refs[1:1 + n_src]
    v_refs = refs[1 + n_src:1 + 2 * n_src]
    o_ref, s0_scr, s1_scr = refs[1 + 2 * n_src:]
    i = pl.program_id(0)

    @pl.when(i == 0)
    def _():
        s1_scr[...] = jnp.zeros(s1_scr.shape, F32)

    @pl.when(i % 2 == 0)
    def _():
        _attn_scores(q_ref, k_refs, s0_scr, tq=tq)
        _attn_softmax_pv(v_refs, o_ref, s1_scr, tq=tq)

    @pl.when(i % 2 == 1)
    def _():
        _attn_scores(q_ref, k_refs, s1_scr, tq=tq)
        _attn_softmax_pv(v_refs, o_ref, s0_scr, tq=tq)


def _attention(q, ks, vs, *, tq):
    b, _, s, _ = q.shape
    t = sum(k.shape[1] for k in ks)
    n_q = s // tq
    n = b * n_q

    def cur(i):
        u = jnp.minimum(i, n - 1)
        return u // n_q, u % n_q

    def prev(i):
        u = jnp.maximum(i - 1, 0)
        return u // n_q, u % n_q

    s_shape = (N_KV_HEADS, GROUP * tq, t)
    k_specs = [pl.BlockSpec((1, k.shape[1], KV_DIM), lambda i: (cur(i)[0], 0, 0)) for k in ks]
    v_specs = [pl.BlockSpec((1, v.shape[1], 2 * KV_DIM), lambda i: (prev(i)[0], 0, 0)) for v in vs]
    return pl.pallas_call(
        functools.partial(_attn_kernel, tq=tq, n_src=len(ks)),
        grid=(n + 1,),
        in_specs=[pl.BlockSpec((1, N_HEADS, tq, HEAD_DIM), lambda i: (cur(i)[0], 0, cur(i)[1], 0))]
        + k_specs + v_specs,
        out_specs=pl.BlockSpec((1, tq, Q_DIM), lambda i: (prev(i)[0], prev(i)[1], 0)),
        out_shape=jax.ShapeDtypeStruct((b, s, Q_DIM), BF16),
        scratch_shapes=[pltpu.VMEM(s_shape, F32), pltpu.VMEM(s_shape, F32)],
        compiler_params=_cparams(1),
        name="attention",
    )(q, *ks, *vs)


def _tile(l):
    return min(l, 512)


def kernel(x_prompt, x_sample, cache_attn_k, cache_attn_v, c, c_ctx, ada_w, ada_b, norm1_g, norm2_g,
           conv_w_pw1, conv_dw_w, conv_dw_b, conv_ln_g, conv_ln_b, conv_w_pw2,
           attn_w_qkv, attn_q_g, attn_k_g, attn_w_o,
           ffn_w_up, ffn_conv_w, ffn_conv_b, ffn_w_down, final_g):
    n_dec = c.shape[0]
    cond = jnp.concatenate(
        [c_ctx[None, :], c, jnp.zeros((COND_ROWS - 1 - n_dec, D_MODEL), F32)], axis=0)
    mod_all = _adaln(cond, ada_w, ada_b)

    def row(v):
        return v.reshape(1, -1)

    w_pw1 = conv_w_pw1[0].astype(BF16)
    w_pw2 = conv_w_pw2[0].astype(BF16)
    w_qkv = attn_w_qkv[0].astype(BF16)
    w_o = attn_w_o[0].astype(BF16)
    w_up = [ffn_w_up[i].astype(BF16) for i in range(DEPTH)]
    w_down = [ffn_w_down[i].astype(BF16) for i in range(DEPTH)]

    def trunk(x, mods, ctx_k, ctx_v):
        b, l, _ = x.shape
        tl = _tile(l)
        x = _conv_mixer(x, mods[0], row(norm1_g[0]), w_pw1, conv_dw_w[0], row(conv_dw_b[0]),
                        row(conv_ln_g[0]), row(conv_ln_b[0]), w_pw2, tl=tl)
        x = _conv_ffn(x, None, None, mods[0], row(norm2_g[0]), w_up[0], ffn_conv_w[0],
                      row(ffn_conv_b[0]), w_down[0], None, tl=tl)
        tabs = None if ctx_k is None else _rope_tables(l)
        outs = _qkv(x, mods[1], row(norm1_g[1]), w_qkv, row(attn_q_g[0]), row(attn_k_g[0]), tabs,
                    tm=tl)
        q, k, v = outs[:3]
        ks, vs = ([k], [v]) if ctx_k is None else ([k, ctx_k], [v, ctx_v])
        attn = _attention(q, ks, vs, tq=min(l, 128 if ctx_k is not None else 256))
        y = _conv_ffn(x, attn, w_o, mods[1], row(norm2_g[1]), w_up[1], ffn_conv_w[1],
                      row(ffn_conv_b[1]), w_down[1], row(final_g), tl=tl)
        return y, outs[3:]

    mods_p = [mod_all[i, 0:1].reshape(1, 1, -1) for i in range(DEPTH)]
    mods_s = [mod_all[i, 1:1 + n_dec].reshape(n_dec, 1, -1) for i in range(DEPTH)]

    y_prompt, (k32, v32) = trunk(x_prompt, mods_p, None, None)
    bp, lp, _ = x_prompt.shape
    new_k = k32.reshape(bp, 1, lp, N_KV_HEADS, HEAD_DIM)
    new_v = v32.reshape(bp, 1, lp, N_KV_HEADS, HEAD_DIM)

    past = cache_attn_k.shape[2]
    ctx_k = cache_attn_k[:, 0].reshape(n_dec, past, KV_DIM).astype(BF16)
    ctx_v = cache_attn_v[:, 0].astype(BF16)
    ctx_v = jnp.concatenate([ctx_v, jnp.ones_like(ctx_v)], axis=-1).reshape(n_dec, past, 2 * KV_DIM)
    y_sample, _ = trunk(x_sample, mods_s, ctx_k, ctx_v)
    return (y_prompt, y_sample, new_k, new_v)
```
